```python
import math
import jax, jax.numpy as jnp
from jax import lax
import numpy as np

D_MODEL = 1024
BATCH = 4
SEQ = 8192
DEPTH = 2
DEC_BATCH = 32
DEC_SEQ = 4
PAST_LEN = 16384
PAGE_SIZE = 128

N_META = 16
CONV_K = 4
CHUNK = 64
Q_BLOCK = 128
EPS = 1e-6
MASK_VALUE = -1e30

SSD_HEADS = 16
SSD_HEAD_DIM = 64
SSD_INNER = SSD_HEADS * SSD_HEAD_DIM
SSD_GROUPS = 2
SSD_STATE = 128
SSD_CONV_DIM = SSD_INNER + 2 * SSD_GROUPS * SSD_STATE

GDN_HEADS = 8
GDN_DK = 128
GDN_DV = 128
GDN_KEY = GDN_HEADS * GDN_DK
GDN_VAL = GDN_HEADS * GDN_DV
GDN_CONV_DIM = 2 * GDN_KEY + GDN_VAL

HYB_SPLITS = [SSD_INNER, SSD_CONV_DIM, SSD_HEADS, GDN_CONV_DIM, GDN_HEADS, GDN_HEADS, GDN_VAL]
HYB_IN = sum(HYB_SPLITS)
HYB_MIX = SSD_INNER + GDN_VAL

FOX_HEADS = 8
FOX_KV_HEADS = 4
FOX_REP = FOX_HEADS // FOX_KV_HEADS
FOX_HEAD_DIM = 128
FOX_SPLITS = [FOX_HEADS * FOX_HEAD_DIM, FOX_KV_HEADS * FOX_HEAD_DIM, FOX_KV_HEADS * FOX_HEAD_DIM, FOX_HEADS]
FOX_IN = sum(FOX_SPLITS)
FOX_MIX = FOX_HEADS * FOX_HEAD_DIM
FORGET_BIAS_INIT = 4.0

D_FF = 4 * D_MODEL
N_HYB = (DEPTH + 1) // 2
N_FOX = DEPTH // 2

kernel_name = "hybrid_ssd_gdn_fox_decoder_step"

F32 = jnp.float32


def split_last(t, sizes):
    return jnp.split(t, np.cumsum(sizes)[:-1].tolist(), axis=-1)


def rms_norm(x, g):
    xf = x.astype(F32)
    y = xf * lax.rsqrt(jnp.mean(xf * xf, axis=-1, keepdims=True) + EPS)
    return (y * g.astype(F32)).astype(x.dtype)


def l2_norm(x):
    xf = x.astype(F32)
    return (xf * lax.rsqrt(jnp.sum(xf * xf, axis=-1, keepdims=True) + EPS)).astype(x.dtype)


def front_pad(t, pad):
    return jnp.pad(t, [(0, 0), (pad, 0)] + [(0, 0)] * (t.ndim - 2))


def causal_conv(u, hist, w, b=None):
    full = jnp.concatenate([hist.astype(u.dtype), u], axis=1)
    T = u.shape[1]
    out = full[:, 0:T] * w[0]
    for j in range(1, CONV_K):
        out = out + full[:, j:j + T] * w[j]
    if b is not None:
        out = out + b
    return jax.nn.silu(out), full[:, T:]


def sq_relu_mlp(u, w_up, w_down):
    return jnp.square(jax.nn.relu(u @ w_up)) @ w_down


def ssd_chunked(x, dt, A, B, C):
    b, L, H, P = x.shape
    G, N = B.shape[2], B.shape[3]
    R = H // G
    nc = L // CHUNK
    xc = (x * dt[..., None]).reshape(b, nc, CHUNK, G, R, P)
    a = (dt * A).astype(F32).reshape(b, nc, CHUNK, G, R)
    Bc = B.reshape(b, nc, CHUNK, G, N)
    Cc = C.reshape(b, nc, CHUNK, G, N)
    a_cum = jnp.cumsum(a, axis=2)
    causal = jnp.tril(jnp.ones((CHUNK, CHUNK), bool))[:, :, None, None]
    seg = a_cum[:, :, :, None] - a_cum[:, :, None, :]
    decay = jnp.exp(jnp.where(causal, seg, -jnp.inf))
    cb = jnp.einsum('bclgn,bcsgn->bclsg', Cc, Bc)
    y_diag = jnp.einsum('bclsgr,bcsgrp->bclgrp', cb[..., None] * decay, xc)
    decay_to_end = jnp.exp(a_cum[:, :, -1:] - a_cum)
    chunk_states = jnp.einsum('bclgn,bclgr,bclgrp->bcgrpn', Bc, decay_to_end, xc)
    chunk_decay = jnp.exp(a_cum[:, :, -1])

    def step(h, inp):
        s_c, d_c = inp
        return h * d_c[..., None, None] + s_c, h

    h0 = jnp.zeros((b, G, R, P, N), F32)
    h_final, h_in = lax.scan(step, h0, (jnp.moveaxis(chunk_states, 1, 0), jnp.moveaxis(chunk_decay, 1, 0)))
    h_in = jnp.moveaxis(h_in, 0, 1)
    y_off = jnp.einsum('bclgn,bcgrpn,bclgr->bclgrp', Cc, h_in, jnp.exp(a_cum))
    y = (y_diag + y_off).reshape(b, L, H, P)
    return y, h_final.reshape(b, H, P, N)


def ssd_recurrent(x, dt, A, B, C, h0):
    H = x.shape[2]
    R = H // B.shape[2]
    Bh = jnp.repeat(B, R, axis=2)
    Ch = jnp.repeat(C, R, axis=2)

    def step(h, inp):
        x_t, dt_t, B_t, C_t = inp
        h = h * jnp.exp(dt_t * A)[..., None, None] + jnp.einsum('bhp,bhn->bhpn', x_t * dt_t[..., None], B_t)
        return h, jnp.einsum('bhpn,bhn->bhp', h, C_t)

    xs = (jnp.moveaxis(x, 1, 0), jnp.moveaxis(dt, 1, 0), jnp.moveaxis(Bh, 1, 0), jnp.moveaxis(Ch, 1, 0))
    h, y = lax.scan(step, h0.astype(F32), xs)
    return jnp.moveaxis(y, 0, 1), h


def gdn_chunked(q, k, v, g, beta):
    b, L, H, Dk = q.shape
    Dv = v.shape[-1]
    nc = L // CHUNK

    def to_chunks(t):
        t = t.reshape((b, nc, CHUNK, H) + t.shape[3:])
        return jnp.moveaxis(t, 3, 2)

    q, k, v = to_chunks(q), to_chunks(k), to_chunks(v)
    g_cum = jnp.cumsum(to_chunks(g.astype(F32)), axis=-1)
    beta = to_chunks(beta)
    causal = jnp.tril(jnp.ones((CHUNK, CHUNK), bool))
    strict = jnp.tril(jnp.ones((CHUNK, CHUNK), bool), -1)
    decay = jnp.exp(jnp.where(causal, g_cum[..., :, None] - g_cum[..., None, :], -jnp.inf))
    k_beta = k * beta[..., None]
    a_mat = jnp.where(strict, jnp.einsum('bchld,bchsd->bchls', k_beta, k) * decay, 0.0)
    rhs = jnp.concatenate([v * beta[..., None], k_beta * jnp.exp(g_cum)[..., None]], axis=-1)
    sol = lax.linalg.triangular_solve(a_mat.astype(F32), rhs.astype(F32), left_side=True, lower=True,
                                      unit_diagonal=True)
    u, w = sol[..., :Dv], sol[..., Dv:]
    qk = jnp.einsum('bchld,bchsd->bchls', q, k) * decay
    q_dec = q * jnp.exp(g_cum)[..., None]
    k_end = k * jnp.exp(g_cum[..., -1:] - g_cum)[..., None]
    chunk_decay = jnp.exp(g_cum[..., -1])

    def step(S, inp):
        u_c, w_c, q_c, qk_c, k_c, d_c = inp
        v_new = u_c - jnp.einsum('bhld,bhdv->bhlv', w_c, S)
        o = jnp.einsum('bhld,bhdv->bhlv', q_c, S) + jnp.einsum('bhls,bhsv->bhlv', qk_c, v_new)
        S = S * d_c[..., None, None] + jnp.einsum('bhld,bhlv->bhdv', k_c, v_new)
        return S, o

    xs = (jnp.moveaxis(u, 1, 0), jnp.moveaxis(w, 1, 0), jnp.moveaxis(q_dec, 1, 0), jnp.moveaxis(qk, 1, 0),
          jnp.moveaxis(k_end, 1, 0), jnp.moveaxis(chunk_decay, 1, 0))
    S, o = lax.scan(step, jnp.zeros((b, H, Dk, Dv), F32), xs)
    o = jnp.moveaxis(jnp.moveaxis(o, 0, 1), 2, 3).reshape(b, L, H, Dv)
    return o, S


def gdn_recurrent(q, k, v, g, beta, S0):
    def step(S, inp):
        q_t, k_t, v_t, g_t, b_t = inp
        S = S * jnp.exp(g_t)[..., None, None]
        v_new = (v_t - jnp.einsum('bhd,bhdv->bhv', k_t, S)) * b_t[..., None]
        S = S + jnp.einsum('bhd,bhv->bhdv', k_t, v_new)
        return S, jnp.einsum('bhd,bhdv->bhv', q_t, S)

    xs = (jnp.moveaxis(q, 1, 0), jnp.moveaxis(k, 1, 0), jnp.moveaxis(v, 1, 0), jnp.moveaxis(g, 1, 0),
          jnp.moveaxis(beta, 1, 0))
    S, o = lax.scan(step, S0.astype(F32), xs)
    return jnp.moveaxis(o, 0, 1), S


def hybrid_mixer(u, ssd_hist, gdn_hist, ssd_h0, gdn_s0, prm, recurrent):
    (w_in, ssd_conv_w, ssd_conv_b, ssd_dt_bias, ssd_a_log, ssd_d, ssd_norm,
     gdn_conv_w, gdn_dt_bias, gdn_a_log, gdn_norm, w_out) = prm
    b, T, _ = u.shape
    z, xbc, dt_raw, qkv, a_raw, b_raw, gate = split_last(u @ w_in, HYB_SPLITS)
    xbc, ssd_hist_new = causal_conv(xbc, ssd_hist, ssd_conv_w, ssd_conv_b)
    xs, Bs, Cs = split_last(xbc, [SSD_INNER, SSD_GROUPS * SSD_STATE, SSD_GROUPS * SSD_STATE])
    xs = xs.reshape(b, T, SSD_HEADS, SSD_HEAD_DIM)
    Bs = Bs.reshape(b, T, SSD_GROUPS, SSD_STATE)
    Cs = Cs.reshape(b, T, SSD_GROUPS, SSD_STATE)
    dt = jax.nn.softplus((dt_raw + ssd_dt_bias).astype(F32))
    A = -jnp.exp(ssd_a_log.astype(F32))
    qkv, gdn_hist_new = causal_conv(qkv, gdn_hist, gdn_conv_w)
    q, k, v = split_last(qkv, [GDN_KEY, GDN_KEY, GDN_VAL])
    q = l2_norm(q.reshape(b, T, GDN_HEADS, GDN_DK)) * (GDN_DK ** -0.5)
    k = l2_norm(k.reshape(b, T, GDN_HEADS, GDN_DK))
    v = v.reshape(b, T, GDN_HEADS, GDN_DV)
    beta = jax.nn.sigmoid(b_raw.astype(F32))
    g = -jnp.exp(gdn_a_log.astype(F32)) * jax.nn.softplus((a_raw + gdn_dt_bias).astype(F32))
    if recurrent:
        y_ssd, ssd_state = ssd_recurrent(xs, dt, A, Bs, Cs, ssd_h0)
        o_gdn, gdn_state = gdn_recurrent(q, k, v, g, beta, gdn_s0)
    else:
        pad = (-N_META) % CHUNK
        y_ssd, ssd_state = ssd_chunked(front_pad(xs, pad), front_pad(dt, pad), A, front_pad(Bs, pad),
                                       front_pad(Cs, pad))
        o_gdn, gdn_state = gdn_chunked(front_pad(q, pad), front_pad(k, pad), front_pad(v, pad),
                                       front_pad(g, pad), front_pad(beta, pad))
        y_ssd, o_gdn = y_ssd[:, pad:], o_gdn[:, pad:]
    y_ssd = y_ssd + xs * ssd_d[:, None]
    yz = (y_ssd.reshape(b, T, SSD_INNER) * jax.nn.silu(z)).reshape(b, T, SSD_GROUPS, SSD_INNER // SSD_GROUPS)
    y_ssd = rms_norm(yz, ssd_norm.reshape(SSD_GROUPS, -1)).reshape(b, T, SSD_INNER)
    o_gdn = rms_norm(o_gdn, gdn_norm) * jax.nn.silu(gate.reshape(b, T, GDN_HEADS, GDN_DV))
    mix = jnp.concatenate([y_ssd, o_gdn.reshape(b, T, GDN_VAL)], axis=-1).astype(u.dtype)
    return mix @ w_out, ssd_hist_new, gdn_hist_new, ssd_state, gdn_state


def fox_project(u, w_in, f_bias, q_norm, k_norm):
    b, T, _ = u.shape
    q, k, v, f = split_last(u @ w_in, FOX_SPLITS)
    q = rms_norm(q.reshape(b, T, FOX_HEADS, FOX_HEAD_DIM), q_norm) * (FOX_HEAD_DIM ** -0.5)
    k = rms_norm(k.reshape(b, T, FOX_KV_HEADS, FOX_HEAD_DIM), k_norm)
    v = v.reshape(b, T, FOX_KV_HEADS, FOX_HEAD_DIM)
    logf = jax.nn.log_sigmoid((f + f_bias).astype(F32))
    return q, k, v, logf


def fox_prompt_attention(q, k, v, logf):
    b, L, H, Dh = q.shape
    pad = (-N_META) % Q_BLOCK
    q, k, v, logf = front_pad(q, pad), front_pad(k, pad), front_pad(v, pad), front_pad(logf, pad)
    Lp = L + pad
    c = jnp.cumsum(logf.astype(F32), axis=1)
    c = jnp.transpose(c.reshape(b, Lp, FOX_KV_HEADS, FOX_REP), (0, 2, 3, 1))
    qg = q.reshape(b, Lp, FOX_KV_HEADS, FOX_REP, Dh)
    key_pos = jnp.arange(Lp)

    def block(i):
        start = i * Q_BLOCK
        qb = lax.dynamic_slice_in_dim(qg, start, Q_BLOCK, axis=1)
        cq = lax.dynamic_slice_in_dim(c, start, Q_BLOCK, axis=3)
        q_pos = start + jnp.arange(Q_BLOCK)
        mask = (key_pos[None, :] <= q_pos[:, None]) & (key_pos[None, :] >= pad)
        s = jnp.einsum('bqgrd,bkgd->bgrqk', qb, k).astype(F32) + (cq[..., :, None] - c[..., None, :])
        p = jax.nn.softmax(jnp.where(mask, s, MASK_VALUE), axis=-1)
        return jnp.einsum('bgrqk,bkgd->bqgrd', p.astype(v.dtype), v)

    o = lax.map(block, jnp.arange(Lp // Q_BLOCK))
    o = jnp.moveaxis(o, 0, 1).reshape(b, Lp, H * Dh)
    return o[:, pad:]


def fox_decode_attention(q, k_new, v_new, logf_new, k_past, v_past, logf_past):
    b, T, H, Dh = q.shape
    P = k_past.shape[1]
    k_all = jnp.concatenate([k_past, k_new.astype(k_past.dtype)], axis=1)
    v_all = jnp.concatenate([v_past, v_new.astype(v_past.dtype)], axis=1)
    rev = lax.cumsum(logf_past.astype(F32), axis=1, reverse=True)
    r_past = jnp.concatenate([rev[:, 1:], jnp.zeros_like(rev[:, :1])], axis=1)
    c_new = jnp.cumsum(logf_new.astype(F32), axis=1)
    key_term = jnp.concatenate([r_past, -c_new], axis=1)

    def to_grk(t):
        return jnp.transpose(t.reshape(b, t.shape[1], FOX_KV_HEADS, FOX_REP), (0, 2, 3, 1))

    bias = to_grk(c_new)[..., :, None] + to_grk(key_term)[..., None, :]
    qg = q.reshape(b, T, FOX_KV_HEADS, FOX_REP, Dh)
    s = jnp.einsum('btgrd,bkgd->bgrtk', qg, k_all).astype(F32) + bias
    mask = jnp.concatenate([jnp.ones((T, P), bool), jnp.tril(jnp.ones((T, T), bool))], axis=1)
    p = jax.nn.softmax(jnp.where(mask, s, MASK_VALUE), axis=-1)
    o = jnp.einsum('bgrtk,bkgd->btgrd', p.astype(v_all.dtype), v_all)
    return o.reshape(b, T, H * Dh)


def setup_inputs(seed: int = 0) -> dict:
    key = jax.random.key(seed)
    ks = iter(jax.random.split(key, 48))

    def nrm(shape, scale=1.0):
        return scale * jax.random.normal(next(ks), shape, F32)

    def unif(shape, lo, hi):
        return jax.random.uniform(next(ks), shape, F32, lo, hi)

    def dt_bias(shape):
        dt0 = jnp.exp(unif(shape, math.log(1e-3), math.log(1e-1)))
        return dt0 + jnp.log(-jnp.expm1(-dt0))

    n_pages = PAST_LEN // PAGE_SIZE
    n_used = DEC_BATCH * n_pages
    n_pool = n_used + n_used // 4
    inp = {}
    inp["x_prompt"] = nrm((BATCH, SEQ, D_MODEL))
    inp["x_sample"] = nrm((DEC_BATCH, DEC_SEQ, D_MODEL))
    inp["state_ssd"] = nrm((N_HYB, DEC_BATCH, SSD_HEADS, SSD_HEAD_DIM, SSD_STATE), 0.1)
    inp["state_ssd_conv"] = nrm((N_HYB, DEC_BATCH, CONV_K - 1, SSD_CONV_DIM))
    inp["state_gdn"] = nrm((N_HYB, DEC_BATCH, GDN_HEADS, GDN_DK, GDN_DV), 0.1)
    inp["state_gdn_conv"] = nrm((N_HYB, DEC_BATCH, CONV_K - 1, GDN_CONV_DIM))
    inp["cache_k"] = nrm((N_FOX, n_pool, PAGE_SIZE, FOX_KV_HEADS, FOX_HEAD_DIM))
    inp["cache_v"] = nrm((N_FOX, n_pool, PAGE_SIZE, FOX_KV_HEADS, FOX_HEAD_DIM))
    inp["cache_logf"] = jax.nn.log_sigmoid(FORGET_BIAS_INIT + nrm((N_FOX, n_pool, PAGE_SIZE, FOX_HEADS)))
    inp["page_table"] = jax.random.permutation(next(ks), n_pool)[:n_used].reshape(DEC_BATCH, n_pages).astype(jnp.int32)
    inp["meta_tokens"] = nrm((N_META, D_MODEL))
    inp["norm_mix"] = 1.0 + nrm((DEPTH, D_MODEL), 0.02)
    inp["norm_mlp"] = 1.0 + nrm((DEPTH, D_MODEL), 0.02)
    inp["w_in_hyb"] = nrm((N_HYB, D_MODEL, HYB_IN), D_MODEL ** -0.5)
    inp["ssd_conv_w"] = nrm((N_HYB, CONV_K, SSD_CONV_DIM), CONV_K ** -0.5)
    inp["ssd_conv_b"] = nrm((N_HYB, SSD_CONV_DIM), 0.02)
    inp["ssd_dt_bias"] = dt_bias((N_HYB, SSD_HEADS))
    inp["ssd_a_log"] = jnp.log(unif((N_HYB, SSD_HEADS), 1.0, 16.0))
    inp["ssd_d"] = 1.0 + nrm((N_HYB, SSD_HEADS), 0.02)
    inp["ssd_norm"] = 1.0 + nrm((N_HYB, SSD_INNER), 0.02)
    inp["gdn_conv_w"] = nrm((N_HYB, CONV_K, GDN_CONV_DIM), CONV_K ** -0.5)
    inp["gdn_dt_bias"] = dt_bias((N_HYB, GDN_HEADS))
    inp["gdn_a_log"] = jnp.log(unif((N_HYB, GDN_HEADS), 1.0, 16.0))
    inp["gdn_norm"] = 1.0 + nrm((N_HYB, GDN_DV), 0.02)
    inp["w_out_hyb"] = nrm((N_HYB, HYB_MIX, D_MODEL), HYB_MIX ** -0.5)
    inp["w_in_fox"] = nrm((N_FOX, D_MODEL, FOX_IN), D_MODEL ** -0.5)
    inp["fox_f_bias"] = FORGET_BIAS_INIT + nrm((N_FOX, FOX_HEADS), 0.5)
    inp["fox_q_norm"] = 1.0 + nrm((N_FOX, FOX_HEAD_DIM), 0.02)
    inp["fox_k_norm"] = 1.0 + nrm((N_FOX, FOX_HEAD_DIM), 0.02)
    inp["w_out_fox"] = nrm((N_FOX, FOX_MIX, D_MODEL), FOX_MIX ** -0.5)
    inp["w_up"] = nrm((DEPTH, D_MODEL, D_FF), D_MODEL ** -0.5)
    inp["w_down"] = nrm((DEPTH, D_FF, D_MODEL), D_FF ** -0.5)
    return inp


def reference(x_prompt, x_sample, state_ssd, state_ssd_conv, state_gdn, state_gdn_conv, cache_k, cache_v,
              cache_logf, page_table, meta_tokens, norm_mix, norm_mlp, w_in_hyb, ssd_conv_w, ssd_conv_b,
              ssd_dt_bias, ssd_a_log, ssd_d, ssd_norm, gdn_conv_w, gdn_dt_bias, gdn_a_log, gdn_norm, w_out_hyb,
              w_in_fox, fox_f_bias, fox_q_norm, fox_k_norm, w_out_fox, w_up, w_down):
    b = x_prompt.shape[0]
    db = x_sample.shape[0]
    hp = jnp.concatenate([jnp.broadcast_to(meta_tokens.astype(x_prompt.dtype)[None], (b, N_META, D_MODEL)),
                          x_prompt], axis=1)
    hs = x_sample
    ssd_p, ssd_s, sconv_p, sconv_s, gdn_p, gdn_s, gconv_p, gconv_s = [], [], [], [], [], [], [], []
    k_p, k_s, v_p, v_s, lf_p, lf_s = [], [], [], [], [], []
    for layer in range(DEPTH):
        i = layer // 2
        up = rms_norm(hp, norm_mix[layer])
        us = rms_norm(hs, norm_mix[layer])
        if layer % 2 == 0:
            prm = (w_in_hyb[i], ssd_conv_w[i], ssd_conv_b[i], ssd_dt_bias[i], ssd_a_log[i], ssd_d[i], ssd_norm[i],
                   gdn_conv_w[i], gdn_dt_bias[i], gdn_a_log[i], gdn_norm[i], w_out_hyb[i])
            mp, scp, gcp, stp, gtp = hybrid_mixer(
                up, jnp.zeros((b, CONV_K - 1, SSD_CONV_DIM), up.dtype),
                jnp.zeros((b, CONV_K - 1, GDN_CONV_DIM), up.dtype), None, None, prm, False)
            ms, scs, gcs, sts, gts = hybrid_mixer(
                us, state_ssd_conv[i], state_gdn_conv[i], state_ssd[i], state_gdn[i], prm, True)
            ssd_p.append(stp); ssd_s.append(sts); sconv_p.append(scp); sconv_s.append(scs)
            gdn_p.append(gtp); gdn_s.append(gts); gconv_p.append(gcp); gconv_s.append(gcs)
        else:
            qp, kp, vp, lfp = fox_project(up, w_in_fox[i], fox_f_bias[i], fox_q_norm[i], fox_k_norm[i])
            mp = fox_prompt_attention(qp, kp, vp, lfp).astype(up.dtype) @ w_out_fox[i]
            qs, kn, vn, lfn = fox_project(us, w_in_fox[i], fox_f_bias[i], fox_q_norm[i], fox_k_norm[i])
            k_past = cache_k[i, page_table].reshape(db, -1, FOX_KV_HEADS, FOX_HEAD_DIM)
            v_past = cache_v[i, page_table].reshape(db, -1, FOX_KV_HEADS, FOX_HEAD_DIM)
            lf_past = cache_logf[i, page_table].reshape(db, -1, FOX_HEADS)
            ms = fox_decode_attention(qs, kn, vn, lfn, k_past, v_past, lf_past).astype(us.dtype) @ w_out_fox[i]
            k_p.append(kp); k_s.append(kn); v_p.append(vp); v_s.append(vn); lf_p.append(lfp); lf_s.append(lfn)
        hp = hp + mp
        hs = hs + ms
        hp = hp + sq_relu_mlp(rms_norm(hp, norm_mlp[layer]), w_up[layer], w_down[layer])
        hs = hs + sq_relu_mlp(rms_norm(hs, norm_mlp[layer]), w_up[layer], w_down[layer])
    y_prompt = hp[:, N_META:]
    y_sample = hs
    return (y_prompt, y_sample, jnp.stack(ssd_p), jnp.stack(ssd_s), jnp.stack(sconv_p), jnp.stack(sconv_s),
            jnp.stack(gdn_p), jnp.stack(gdn_s), jnp.stack(gconv_p), jnp.stack(gconv_s),
            jnp.stack(k_p), jnp.stack(k_s), jnp.stack(v_p), jnp.stack(v_s), jnp.stack(lf_p), jnp.stack(lf_s))
```

```python
import functools
import math

import jax
import jax.numpy as jnp
from jax import lax
from jax.experimental import pallas as pl
from jax.experimental.pallas import tpu as pltpu

F32 = jnp.float32
BF16 = jnp.bfloat16

EPS = 1e-6
MASK_VALUE = -1e30
N_META = 16
CONV_K = 4
CHUNK = 64

SSD_HEADS = 16
SSD_HEAD_DIM = 64
SSD_INNER = SSD_HEADS * SSD_HEAD_DIM
SSD_GROUPS = 2
SSD_STATE = 128
SSD_BC = 2 * SSD_GROUPS * SSD_STATE
SSD_CONV_DIM = SSD_INNER + SSD_BC

GDN_HEADS = 8
GDN_DK = 128
GDN_DV = 128
GDN_KEY = GDN_HEADS * GDN_DK
GDN_VAL = GDN_HEADS * GDN_DV
GDN_CONV_DIM = 2 * GDN_KEY + GDN_VAL

FOX_HEADS = 8
FOX_KV_HEADS = 4
FOX_REP = FOX_HEADS // FOX_KV_HEADS
FOX_HEAD_DIM = 128
FOX_Q = FOX_HEADS * FOX_HEAD_DIM
FOX_KV = FOX_KV_HEADS * FOX_HEAD_DIM

LANES = 128
VMEM_LIMIT = 52 * 1024 * 1024

DT_LANE = 0
GA_LANE = SSD_HEADS
GB_LANE = SSD_HEADS + GDN_HEADS

SCAN_ROWS = 128
PAGES_PER_STEP = 8
FOX_TILE = 512


def _params(*sem):
    return pltpu.CompilerParams(dimension_semantics=sem, vmem_limit_bytes=VMEM_LIMIT)


def _row_tile(n, cap):
    best = None
    for t in range(8, min(n, cap) + 1, 8):
        if n % t == 0:
            best = t
    assert best is not None, n
    return best


def _mm(a, b):
    return jnp.dot(a.astype(BF16), b.astype(BF16), preferred_element_type=F32)


def _mm_nt(a, b):
    return lax.dot_general(a.astype(BF16), b.astype(BF16), (((1,), (1,)), ((), ())),
                           preferred_element_type=F32)


def _mm_tn(a, b):
    return lax.dot_general(a.astype(BF16), b.astype(BF16), (((0,), (0,)), ((), ())),
                           preferred_element_type=F32)


def _split3(x):
    hi = x.astype(BF16)
    r = x - hi.astype(F32)
    mid = r.astype(BF16)
    lo = (r - mid.astype(F32)).astype(BF16)
    return hi, mid, lo


def _sel_l(m, x):
    hi, mid, lo = _split3(x)
    d = functools.partial(jnp.dot, preferred_element_type=F32)
    return (d(m, hi) + d(m, mid)) + d(m, lo)


def _sel_r(x, m):
    hi, mid, lo = _split3(x)
    d = functools.partial(jnp.dot, preferred_element_type=F32)
    return (d(hi, m) + d(mid, m)) + d(lo, m)


def _softplus(x):
    return jnp.maximum(x, 0.0) + jnp.log(1.0 + jnp.exp(-jnp.abs(x)))


def _log_sigmoid(x):
    return jnp.minimum(x, 0.0) - jnp.log(1.0 + jnp.exp(-jnp.abs(x)))


def _silu(x):
    return x * jax.nn.sigmoid(x)


def _chunk_tri(rows, lower):
    r = lax.broadcasted_iota(jnp.int32, (rows, rows), 0)
    c = lax.broadcasted_iota(jnp.int32, (rows, rows), 1)
    same = (r // CHUNK) == (c // CHUNK)
    tri = (c <= r) if lower else (r <= c)
    return jnp.where(same & tri, 1.0, 0.0).astype(BF16)


def _rms_matmul_kernel(x_ref, g_ref, w_ref, wt_ref, o_ref, ot_ref, xn_ref):
    @pl.when(pl.program_id(1) == 0)
    def _():
        x = x_ref[...]
        ms = jnp.mean(x * x, axis=-1, keepdims=True)
        xn = (x * lax.rsqrt(ms + EPS) * g_ref[...]).astype(BF16)
        xn_ref[...] = xn
        ot_ref[...] = jnp.dot(xn, wt_ref[...], preferred_element_type=F32)

    o_ref[...] = jnp.dot(xn_ref[...], w_ref[...], preferred_element_type=F32)


def _rms_matmul(x, g, w_main, w_tail, tn):
    n, d = x.shape
    fm = w_main.shape[1]
    tm = _row_tile(n, 1216)
    return pl.pallas_call(
        _rms_matmul_kernel,
        grid=(n // tm, fm // tn),
        in_specs=[
            pl.BlockSpec((tm, d), lambda i, j: (i, 0)),
            pl.BlockSpec((1, d), lambda i, j: (0, 0)),
            pl.BlockSpec((d, tn), lambda i, j: (0, j)),
            pl.BlockSpec((d, LANES), lambda i, j: (0, 0)),
        ],
        out_specs=[
            pl.BlockSpec((tm, tn), lambda i, j: (i, j)),
            pl.BlockSpec((tm, LANES), lambda i, j: (i, 0)),
        ],
        out_shape=[jax.ShapeDtypeStruct((n, fm), F32), jax.ShapeDtypeStruct((n, LANES), F32)],
        scratch_shapes=[pltpu.VMEM((tm, d), BF16)],
        compiler_params=_params("parallel", "arbitrary"),
        name="rms_matmul",
    )(x, g.reshape(1, d), w_main, w_tail)


def _mix_mlp_kernel(n_in, *refs):
    res_ref = refs[0]
    a_refs = refs[1:1 + n_in]
    w_refs = refs[1 + n_in:1 + 2 * n_in]
    g_ref, wup_ref, wdn_ref, o_ref, hn_ref = refs[1 + 2 * n_in:]

    @pl.when(pl.program_id(1) == 0)
    def _():
        h = res_ref[...]
        for a_ref, w_ref in zip(a_refs, w_refs):
            h = h + jnp.dot(a_ref[...], w_ref[...], preferred_element_type=F32)
        ms = jnp.mean(h * h, axis=-1, keepdims=True)
        hn_ref[...] = (h * lax.rsqrt(ms + EPS) * g_ref[...]).astype(BF16)
        o_ref[...] = h

    t = jnp.dot(hn_ref[...], wup_ref[...], preferred_element_type=F32)
    t = jnp.square(jnp.maximum(t, 0.0)).astype(BF16)
    o_ref[...] += jnp.dot(t, wdn_ref[...], preferred_element_type=F32)


def _mix_mlp(res, mixes, w_outs, g, w_up, w_down):
    n, d = res.shape
    dff = w_up.shape[1]
    tm = _row_tile(n, 864)
    tf = 1024
    n_in = len(mixes)
    in_specs = [pl.BlockSpec((tm, d), lambda i, j: (i, 0))]
    in_specs += [pl.BlockSpec((tm, m.shape[1]), lambda i, j: (i, 0)) for m in mixes]
    in_specs += [pl.BlockSpec(w.shape, lambda i, j: (0, 0)) for w in w_outs]
    in_specs += [
        pl.BlockSpec((1, d), lambda i, j: (0, 0)),
        pl.BlockSpec((d, tf), lambda i, j: (0, j)),
        pl.BlockSpec((tf, d), lambda i, j: (j, 0)),
    ]
    return pl.pallas_call(
        functools.partial(_mix_mlp_kernel, n_in),
        grid=(n // tm, dff // tf),
        in_specs=in_specs,
        out_specs=pl.BlockSpec((tm, d), lambda i, j: (i, 0)),
        out_shape=jax.ShapeDtypeStruct((n, d), F32),
        scratch_shapes=[pltpu.VMEM((tm, d), BF16)],
        compiler_params=_params("parallel", "arbitrary"),
        name="mix_mlp",
    )(res, *mixes, *w_outs, g.reshape(1, d), w_up, w_down)


def _conv_silu(buf_ref, u, w_ref, bias, rows):
    buf_ref[8:8 + rows, :] = u
    acc = buf_ref[pl.ds(8 - (CONV_K - 1), rows), :] * w_ref[0:1, :]
    for j in range(1, CONV_K):
        acc = acc + buf_ref[pl.ds(8 - (CONV_K - 1) + j, rows), :] * w_ref[j:j + 1, :]
    if bias is not None:
        acc = acc + bias
    buf_ref[0:8, :] = buf_ref[rows:rows + 8, :]
    return _silu(acc)


def _ssd_kernel(first_valid, seq_len, has_init, *refs):
    if has_init:
        (xr_ref, bcr_ref, z_ref, sm_ref, wx_ref, wbc_ref, bx_ref, bbc_ref, dtb_ref, alog_ref, dexp_ref,
         nrm_ref, e_ref, s0_ref, y_ref, st_ref, bufx_ref, bufb_ref, s_ref) = refs
    else:
        (xr_ref, bcr_ref, z_ref, sm_ref, wx_ref, wbc_ref, bx_ref, bbc_ref, dtb_ref, alog_ref, dexp_ref,
         nrm_ref, e_ref, y_ref, st_ref, bufx_ref, bufb_ref, s_ref) = refs
        s0_ref = None
    rows = SCAN_ROWS
    i = pl.program_id(1)
    gw = SSD_INNER // SSD_GROUPS

    @pl.when(i == 0)
    def _():
        bufx_ref[0:8, :] = jnp.zeros((8, SSD_INNER), F32)
        bufb_ref[0:8, :] = jnp.zeros((8, SSD_BC), F32)
        if has_init:
            s_ref[...] = s0_ref[...].T
        else:
            s_ref[...] = jnp.zeros(s_ref.shape, F32)

    def valid(width):
        r = i * rows + lax.broadcasted_iota(jnp.int32, (rows, width), 0)
        return (r >= first_valid) & (r < seq_len)

    xs = jnp.where(valid(SSD_INNER), _conv_silu(bufx_ref, xr_ref[...], wx_ref, bx_ref[...], rows), 0.0)
    bcs = jnp.where(valid(SSD_BC), _conv_silu(bufb_ref, bcr_ref[...], wbc_ref, bbc_ref[...], rows), 0.0)
    b_all = bcs[:, :SSD_GROUPS * SSD_STATE]
    c_all = bcs[:, SSD_GROUPS * SSD_STATE:]

    dt = jnp.where(valid(LANES), _softplus(sm_ref[...] + dtb_ref[...]), 0.0)
    a = dt * (-jnp.exp(alog_ref[...]))
    e_mat = e_ref[...]
    acum = _sel_l(_chunk_tri(rows, True), a)
    acum_t = _sel_r(a.T, _chunk_tri(rows, False))
    dt_e = _sel_r(dt, e_mat)
    ac_e = _sel_r(acum, e_mat)
    xdt = xs * dt_e
    exp_ac = jnp.exp(ac_e)

    li = lax.broadcasted_iota(jnp.int32, (CHUNK, CHUNK), 0)
    si = lax.broadcasted_iota(jnp.int32, (CHUNK, CHUNK), 1)
    causal = si <= li
    lane_lo = lax.broadcasted_iota(jnp.int32, (CHUNK, LANES), 1) < SSD_HEAD_DIM

    for c in range(rows // CHUNK):
        r0 = c * CHUNK
        ac_c = ac_e[r0:r0 + CHUNK, :]
        ac_last = ac_e[r0 + CHUNK - 1:r0 + CHUNK, :]
        xd = xdt[r0:r0 + CHUNK, :] * jnp.exp(ac_last - ac_c)
        cdec = jnp.exp(ac_last)
        y_parts = []
        for g in range(SSD_GROUPS):
            bg = b_all[r0:r0 + CHUNK, g * SSD_STATE:(g + 1) * SSD_STATE]
            cg = c_all[r0:r0 + CHUNK, g * SSD_STATE:(g + 1) * SSD_STATE]
            cb = _mm_nt(cg, bg)
            s_g = s_ref[:, g * gw:(g + 1) * gw]
            y_off = _mm(cg, s_g) * exp_ac[r0:r0 + CHUNK, g * gw:(g + 1) * gw]
            diag = []
            for jp in range(gw // LANES):
                h0 = (g * gw) // SSD_HEAD_DIM + 2 * jp
                xp = xdt[r0:r0 + CHUNK, g * gw + jp * LANES:g * gw + (jp + 1) * LANES]
                ys = []
                for h in (h0, h0 + 1):
                    seg = acum[r0:r0 + CHUNK, h:h + 1] - acum_t[h:h + 1, r0:r0 + CHUNK]
                    dec = jnp.where(causal, jnp.exp(jnp.minimum(seg, 0.0)), 0.0)
                    ys.append(_mm(cb * dec, xp))
                diag.append(jnp.where(lane_lo, ys[0], ys[1]))
            y_parts.append(jnp.concatenate(diag, axis=1) + y_off)
            s_ref[:, g * gw:(g + 1) * gw] = (s_g * cdec[:, g * gw:(g + 1) * gw]
                                             + _mm_tn(bg, xd[:, g * gw:(g + 1) * gw]))
        y = jnp.concatenate(y_parts, axis=1) + xs[r0:r0 + CHUNK, :] * dexp_ref[...]
        yz = y * _silu(z_ref[r0:r0 + CHUNK, :])
        for g in range(SSD_GROUPS):
            yg = yz[:, g * gw:(g + 1) * gw]
            ms = jnp.mean(yg * yg, axis=-1, keepdims=True)
            y_ref[r0:r0 + CHUNK, g * gw:(g + 1) * gw] = (
                yg * lax.rsqrt(ms + EPS) * nrm_ref[:, g * gw:(g + 1) * gw]).astype(BF16)

    @pl.when(i == pl.num_programs(1) - 1)
    def _():
        st_ref[...] = s_ref[...].T


def _ssd_scan(xr, bcr, z, sm, wts, first_valid, seq_len, s0):
    b, l = xr[0].shape[0], xr[0].shape[1]
    rows = SCAN_ROWS
    nblk = pl.cdiv(l, rows)

    def seq_spec(width, blk):
        return pl.BlockSpec((None, rows, width), lambda bi, i: (bi, i, blk))

    def const_spec(arr):
        return pl.BlockSpec(arr.shape, lambda bi, i: (0,) * arr.ndim)

    ins = [xr[0], bcr[0], z[0], sm[0]] + list(wts)
    in_specs = [seq_spec(SSD_INNER, xr[1]), seq_spec(SSD_BC, bcr[1]), seq_spec(SSD_INNER, z[1]),
                seq_spec(LANES, sm[1])] + [const_spec(w) for w in wts]
    if s0 is not None:
        ins.append(s0)
        in_specs.append(pl.BlockSpec((None, SSD_INNER, SSD_STATE), lambda bi, i: (bi, 0, 0)))
    return pl.pallas_call(
        functools.partial(_ssd_kernel, first_valid, seq_len, s0 is not None),
        grid=(b, nblk),
        in_specs=in_specs,
        out_specs=[
            pl.BlockSpec((None, rows, SSD_INNER), lambda bi, i: (bi, i, 0)),
            pl.BlockSpec((None, SSD_INNER, SSD_STATE), lambda bi, i: (bi, 0, 0)),
        ],
        out_shape=[jax.ShapeDtypeStruct((b, l, SSD_INNER), BF16),
                   jax.ShapeDtypeStruct((b, SSD_INNER, SSD_STATE), F32)],
        scratch_shapes=[pltpu.VMEM((rows + 8, SSD_INNER), F32), pltpu.VMEM((rows + 8, SSD_BC), F32),
                        pltpu.VMEM((SSD_STATE, SSD_INNER), F32)],
        compiler_params=_params("parallel", "arbitrary"),
        name="ssd_scan",
    )(*ins)


def _unit_lower_inverse(a_strict, eye):
    m = -a_strict
    p = eye + m
    for _ in range(int(math.log2(CHUNK)) - 1):
        m = _mm(m, m)
        p = p + _mm(p, m)
    return p


def _gdn_kernel(first_valid, seq_len, has_init, *refs):
    if has_init:
        (qkv_ref, gate_ref, sm_ref, wc_ref, gdtb_ref, galog_ref, gnrm_ref, eg_ref, eb_ref, s0_ref,
         o_ref, st_ref, buf_ref, s_ref) = refs
    else:
        (qkv_ref, gate_ref, sm_ref, wc_ref, gdtb_ref, galog_ref, gnrm_ref, eg_ref, eb_ref,
         o_ref, st_ref, buf_ref, s_ref) = refs
        s0_ref = None
    rows = SCAN_ROWS
    i = pl.program_id(1)
    d = GDN_DK

    @pl.when(i == 0)
    def _():
        buf_ref[0:8, :] = jnp.zeros((8, GDN_CONV_DIM), F32)
        if has_init:
            s_ref[...] = s0_ref[...]
        else:
            s_ref[...] = jnp.zeros(s_ref.shape, F32)

    def valid(width):
        r = i * rows + lax.broadcasted_iota(jnp.int32, (rows, width), 0)
        return (r >= first_valid) & (r < seq_len)

    qkv = jnp.where(valid(GDN_CONV_DIM), _conv_silu(buf_ref, qkv_ref[...], wc_ref, None, rows), 0.0)
    sm = sm_ref[...]
    gl = jnp.where(valid(LANES), -jnp.exp(galog_ref[...]) * _softplus(sm + gdtb_ref[...]), 0.0)
    beta = jnp.where(valid(LANES), jax.nn.sigmoid(sm), 0.0)
    gcum = _sel_l(_chunk_tri(rows, True), gl)
    gcum_t = _sel_r(gl.T, _chunk_tri(rows, False))
    gc_e = _sel_r(gcum, eg_ref[...])
    beta_e = _sel_r(beta, eb_ref[...])
    exp_g = jnp.exp(gc_e)

    li = lax.broadcasted_iota(jnp.int32, (CHUNK, CHUNK), 0)
    si = lax.broadcasted_iota(jnp.int32, (CHUNK, CHUNK), 1)
    causal = si <= li
    strict = si < li
    eye = jnp.where(si == li, 1.0, 0.0).astype(F32)

    for h in range(GDN_HEADS):
        hl = h * d
        qh = qkv[:, hl:hl + d]
        kh = qkv[:, GDN_KEY + hl:GDN_KEY + hl + d]
        vh = qkv[:, 2 * GDN_KEY + hl:2 * GDN_KEY + hl + d]
        qh = (qh * lax.rsqrt(jnp.sum(qh * qh, axis=-1, keepdims=True) + EPS)) * (GDN_DK ** -0.5)
        kh = kh * lax.rsqrt(jnp.sum(kh * kh, axis=-1, keepdims=True) + EPS)
        bh = beta_e[:, hl:hl + d]
        gh = gc_e[:, hl:hl + d]
        eh = exp_g[:, hl:hl + d]
        kb = kh * bh
        for c in range(rows // CHUNK):
            r0 = c * CHUNK
            sl = slice(r0, r0 + CHUNK)
            seg = gh[sl, :CHUNK] - gcum_t[GA_LANE + h:GA_LANE + h + 1, r0:r0 + CHUNK]
            dec = jnp.where(causal, jnp.exp(jnp.minimum(seg, 0.0)), 0.0)
            k_c = kh[sl]
            a_mat = jnp.where(strict, _mm_nt(kb[sl], k_c) * dec, 0.0)
            t_inv = _unit_lower_inverse(a_mat, eye)
            u = _mm(t_inv, vh[sl] * bh[sl])
            w = _mm(t_inv, kb[sl] * eh[sl])
            qk = _mm_nt(qh[sl], k_c) * dec
            g_last = gh[r0 + CHUNK - 1:r0 + CHUNK, :]
            k_end = k_c * jnp.exp(g_last - gh[sl])
            s_h = s_ref[h]
            v_new = u - _mm(w, s_h)
            o_h = _mm(qh[sl] * eh[sl], s_h) + _mm(qk, v_new)
            s_ref[h] = s_h * jnp.exp(g_last) + _mm_tn(k_end, v_new)
            ms = jnp.mean(o_h * o_h, axis=-1, keepdims=True)
            o_n = o_h * lax.rsqrt(ms + EPS) * gnrm_ref[...]
            o_ref[sl, hl:hl + d] = (o_n * _silu(gate_ref[sl, hl:hl + d])).astype(BF16)

    @pl.when(i == pl.num_programs(1) - 1)
    def _():
        st_ref[...] = s_ref[...]


def _gdn_scan(qkv, gate, sm, wts, first_valid, seq_len, s0):
    b, l = qkv[0].shape[0], qkv[0].shape[1]
    rows = SCAN_ROWS
    nblk = pl.cdiv(l, rows)

    def seq_spec(width, blk):
        return pl.BlockSpec((None, rows, width), lambda bi, i: (bi, i, blk))

    def const_spec(arr):
        return pl.BlockSpec(arr.shape, lambda bi, i: (0,) * arr.ndim)

    ins = [qkv[0], gate[0], sm[0]] + list(wts)
    in_specs = [seq_spec(GDN_CONV_DIM, qkv[1]), seq_spec(GDN_VAL, gate[1]), seq_spec(LANES, sm[1])]
    in_specs += [const_spec(w) for w in wts]
    state_spec = pl.BlockSpec((None, GDN_HEADS, GDN_DK, GDN_DV), lambda bi, i: (bi, 0, 0, 0))
    if s0 is not None:
        ins.append(s0)
        in_specs.append(state_spec)
    return pl.pallas_call(
        functools.partial(_gdn_kernel, first_valid, seq_len, s0 is not None),
        grid=(b, nblk),
        in_specs=in_specs,
        out_specs=[pl.BlockSpec((None, rows, GDN_VAL), lambda bi, i: (bi, i, 0)), state_spec],
        out_shape=[jax.ShapeDtypeStruct((b, l, GDN_VAL), BF16),
                   jax.ShapeDtypeStruct((b, GDN_HEADS, GDN_DK, GDN_DV), F32)],
        scratch_shapes=[pltpu.VMEM((rows + 8, GDN_CONV_DIM), F32),
                        pltpu.VMEM((GDN_HEADS, GDN_DK, GDN_DV), F32)],
        compiler_params=_params("parallel", "arbitrary"),
        name="gdn_scan",
    )(*ins)


def _fox_prep_kernel(seq_len, seg, qkv_ref, ft_ref, fb_ref, qw_ref, kw_ref,
                     qn_ref, kn_ref, kb_ref, vb_ref, lf_ref, c_ref, ct_ref, carry_ref):
    rows = qkv_ref.shape[0]
    i = pl.program_id(1)

    @pl.when(i == 0)
    def _():
        carry_ref[...] = jnp.zeros(carry_ref.shape, F32)

    d = FOX_HEAD_DIM
    for h in range(FOX_HEADS):
        qh = qkv_ref[:, h * d:(h + 1) * d]
        ms = jnp.mean(qh * qh, axis=-1, keepdims=True)
        qn_ref[:, h * d:(h + 1) * d] = ((qh * lax.rsqrt(ms + EPS) * qw_ref[...]) * (d ** -0.5)).astype(BF16)
    for g in range(FOX_KV_HEADS):
        kh = qkv_ref[:, FOX_Q + g * d:FOX_Q + (g + 1) * d]
        ms = jnp.mean(kh * kh, axis=-1, keepdims=True)
        kn = kh * lax.rsqrt(ms + EPS) * kw_ref[...]
        kn_ref[:, g * d:(g + 1) * d] = kn
        kb_ref[:, g * d:(g + 1) * d] = kn.astype(BF16)
    vb_ref[...] = qkv_ref[:, FOX_Q + FOX_KV:].astype(BF16)

    r = i * rows + lax.broadcasted_iota(jnp.int32, (rows, LANES), 0)
    lf = jnp.where(r < seq_len, _log_sigmoid(ft_ref[...] + fb_ref[...]), 0.0)
    lf_ref[...] = lf
    rr = lax.broadcasted_iota(jnp.int32, (rows, rows), 0)
    cc = lax.broadcasted_iota(jnp.int32, (rows, rows), 1)
    if seg >= rows:
        tri = jnp.where(cc <= rr, 1.0, 0.0).astype(BF16)
        c = _sel_l(tri, lf) + carry_ref[...]
        carry_ref[...] = c[rows - 1:rows, :]
    else:
        tri = jnp.where((cc <= rr) & ((cc // seg) == (rr // seg)), 1.0, 0.0).astype(BF16)
        c = _sel_l(tri, lf)
    c_ref[...] = c
    ct_ref[...] = c.T[0:FOX_HEADS, :]


def _fox_prep(proj, ftail, f_bias, q_norm, k_norm, rows, seg):
    b, l, _ = proj.shape
    nblk = pl.cdiv(l, rows)
    fb = jnp.zeros((1, LANES), F32).at[0, :FOX_HEADS].set(f_bias)

    def seq_spec(width):
        return pl.BlockSpec((None, rows, width), lambda bi, i: (bi, i, 0))

    def const_spec(n):
        return pl.BlockSpec((1, n), lambda bi, i: (0, 0))

    return pl.pallas_call(
        functools.partial(_fox_prep_kernel, l, seg),
        grid=(b, nblk),
        in_specs=[seq_spec(FOX_Q + 2 * FOX_KV), seq_spec(LANES), const_spec(LANES), const_spec(FOX_HEAD_DIM),
                  const_spec(FOX_HEAD_DIM)],
        out_specs=[seq_spec(FOX_Q), seq_spec(FOX_KV), seq_spec(FOX_KV), seq_spec(FOX_KV), seq_spec(LANES),
                   seq_spec(LANES), pl.BlockSpec((None, FOX_HEADS, rows), lambda bi, i: (bi, 0, i))],
        out_shape=[jax.ShapeDtypeStruct((b, l, FOX_Q), BF16), jax.ShapeDtypeStruct((b, l, FOX_KV), F32),
                   jax.ShapeDtypeStruct((b, l, FOX_KV), BF16), jax.ShapeDtypeStruct((b, l, FOX_KV), BF16),
                   jax.ShapeDtypeStruct((b, l, LANES), F32), jax.ShapeDtypeStruct((b, l, LANES), F32),
                   jax.ShapeDtypeStruct((b, FOX_HEADS, l), F32)],
        scratch_shapes=[pltpu.VMEM((1, LANES), F32)],
        compiler_params=_params("parallel", "arbitrary"),
        name="fox_prep",
    )(proj, ftail, fb, q_norm.reshape(1, -1), k_norm.reshape(1, -1))


def _fox_flash_kernel(seq_len, q_ref, k_ref, v_ref, c_ref, ct_ref, o_ref, m_ref, l_ref, acc_ref):
    t = q_ref.shape[0]
    i = pl.program_id(1)
    j = pl.program_id(2)
    d = FOX_HEAD_DIM

    @pl.when(j == 0)
    def _():
        m_ref[...] = jnp.full(m_ref.shape, MASK_VALUE, F32)
        l_ref[...] = jnp.zeros(l_ref.shape, F32)
        acc_ref[...] = jnp.zeros(acc_ref.shape, F32)

    def step(diagonal):
        if diagonal:
            qpos = i * t + lax.broadcasted_iota(jnp.int32, (t, t), 0)
            kpos = j * t + lax.broadcasted_iota(jnp.int32, (t, t), 1)
            keep = kpos <= qpos
            krow = j * t + lax.broadcasted_iota(jnp.int32, (t, d), 0)
        for h in range(FOX_HEADS):
            g = h // FOX_REP
            q = q_ref[:, h * d:(h + 1) * d]
            k = k_ref[:, g * d:(g + 1) * d]
            v = v_ref[:, g * d:(g + 1) * d]
            s = lax.dot_general(q, k, (((1,), (1,)), ((), ())), preferred_element_type=F32)
            s = s + (c_ref[:, h:h + 1] - ct_ref[h:h + 1, :])
            if diagonal:
                s = jnp.where(keep, s, MASK_VALUE)
                v = jnp.where(krow < seq_len, v, jnp.zeros_like(v))
            m_old = m_ref[h]
            m_new = jnp.maximum(m_old, jnp.max(s, axis=-1, keepdims=True))
            p = jnp.exp(s - m_new[:, 0:1])
            alpha = jnp.exp(m_old - m_new)
            l_ref[h] = alpha * l_ref[h] + jnp.sum(p, axis=-1, keepdims=True)
            acc_ref[h] = alpha * acc_ref[h] + jnp.dot(p.astype(BF16), v, preferred_element_type=F32)
            m_ref[h] = m_new

    @pl.when(j < i)
    def _():
        step(False)

    @pl.when(j == i)
    def _():
        step(True)
        for h in range(FOX_HEADS):
            o_ref[:, h * d:(h + 1) * d] = (acc_ref[h] / l_ref[h]).astype(BF16)


def _fox_flash(qn, kb, vb, c, ct, t):
    b, l, _ = qn.shape
    nblk = pl.cdiv(l, t)
    kv_spec = pl.BlockSpec((None, t, FOX_KV), lambda bi, i, j: (bi, jnp.minimum(i, j), 0))
    return pl.pallas_call(
        functools.partial(_fox_flash_kernel, l),
        grid=(b, nblk, nblk),
        in_specs=[
            pl.BlockSpec((None, t, FOX_Q), lambda bi, i, j: (bi, i, 0)),
            kv_spec, kv_spec,
            pl.BlockSpec((None, t, LANES), lambda bi, i, j: (bi, i, 0)),
            pl.BlockSpec((None, FOX_HEADS, t), lambda bi, i, j: (bi, 0, jnp.minimum(i, j))),
        ],
        out_specs=pl.BlockSpec((None, t, FOX_Q), lambda bi, i, j: (bi, i, 0)),
        out_shape=jax.ShapeDtypeStruct((b, l, FOX_Q), BF16),
        scratch_shapes=[pltpu.VMEM((FOX_HEADS, t, LANES), F32), pltpu.VMEM((FOX_HEADS, t, LANES), F32),
                        pltpu.VMEM((FOX_HEADS, t, FOX_HEAD_DIM), F32)],
        compiler_params=_params("parallel", "parallel", "arbitrary"),
        name="fox_flash",
    )(qn, kb, vb, c, ct)


def _fox_decode_kernel(n_tok, pt_ref, q_ref, cq_ref, kn_ref, vn_ref, ktn_ref, *refs):
    pp = PAGES_PER_STEP
    k_refs = refs[0:pp]
    v_refs = refs[pp:2 * pp]
    lf_refs = refs[2 * pp:3 * pp]
    o_ref, m_ref, l_ref, acc_ref, carry_ref = refs[3 * pp:]
    j = pl.program_id(1)
    nrow = q_ref.shape[0]
    page = k_refs[0].shape[0]
    q = q_ref[...]
    cq = cq_ref[...]

    def update(s, pv_fn):
        m_old = m_ref[...]
        m_new = jnp.maximum(m_old, jnp.max(s, axis=-1, keepdims=True))
        p = jnp.exp(s - m_new[:, 0:1])
        alpha = jnp.exp(m_old - m_new)
        l_ref[...] = alpha * l_ref[...] + jnp.sum(p, axis=-1, keepdims=True)
        acc_ref[...] = alpha[:, 0:1] * acc_ref[...] + pv_fn(p.astype(BF16))
        m_ref[...] = m_new

    @pl.when(j == 0)
    def _():
        m_ref[...] = jnp.full(m_ref.shape, MASK_VALUE, F32)
        l_ref[...] = jnp.zeros(l_ref.shape, F32)
        acc_ref[...] = jnp.zeros(acc_ref.shape, F32)
        carry_ref[...] = jnp.zeros(carry_ref.shape, F32)
        s = lax.dot_general(q, kn_ref[...], (((1,), (1,)), ((), ())), preferred_element_type=F32)
        s = s + (cq + ktn_ref[...])
        tok = lax.broadcasted_iota(jnp.int32, (nrow, page), 0) // FOX_HEADS
        key = lax.broadcasted_iota(jnp.int32, (nrow, page), 1)
        s = jnp.where(key <= tok, s, MASK_VALUE)
        update(s, lambda p: jnp.dot(p, vn_ref[...], preferred_element_type=F32))

    jj = lax.broadcasted_iota(jnp.int32, (page, page), 0)
    ss = lax.broadcasted_iota(jnp.int32, (page, page), 1)
    later = jnp.where(jj > ss, 1.0, 0.0).astype(BF16)
    carry = carry_ref[...]
    s_parts = []
    for p_i in range(pp):
        lf_t = lf_refs[p_i][...]
        r = _sel_r(lf_t, later) + carry
        carry = carry + jnp.sum(lf_t, axis=-1, keepdims=True)
        bias = jnp.concatenate([r] * n_tok, axis=0)
        s = lax.dot_general(q, k_refs[p_i][...].astype(BF16), (((1,), (1,)), ((), ())),
                            preferred_element_type=F32)
        s_parts.append(s + (cq + bias))
    carry_ref[...] = carry
    s_all = jnp.concatenate(s_parts, axis=1)

    def pv(p):
        out = jnp.dot(p[:, 0:page], v_refs[0][...].astype(BF16), preferred_element_type=F32)
        for p_i in range(1, pp):
            out = out + jnp.dot(p[:, p_i * page:(p_i + 1) * page], v_refs[p_i][...].astype(BF16),
                                preferred_element_type=F32)
        return out

    update(s_all, pv)

    @pl.when(j == pl.num_programs(1) - 1)
    def _():
        o_ref[...] = acc_ref[...] / l_ref[:, 0:1]


def _fox_decode(page_table, q_all, cq, k_new, v_new, kt_new, cache_k, cache_v, cache_lft, n_tok):
    b, nrow, _ = q_all.shape
    n_pages = page_table.shape[1]
    page = cache_k.shape[1]
    pp = PAGES_PER_STEP
    assert n_pages % pp == 0
    nsteps = n_pages // pp

    def bspec(shape):
        return pl.BlockSpec((None,) + shape, lambda bi, j, pt: (bi, 0, 0))

    def page_spec(shape, p_i):
        return pl.BlockSpec((None,) + shape,
                            lambda bi, j, pt: (pt[bi, n_pages - 1 - (j * pp + p_i)], 0, 0))

    in_specs = [bspec((nrow, FOX_KV)), bspec((nrow, LANES)), bspec((page, FOX_KV)), bspec((page, FOX_KV)),
                bspec((nrow, page))]
    in_specs += [page_spec((page, FOX_KV), p_i) for p_i in range(pp)]
    in_specs += [page_spec((page, FOX_KV), p_i) for p_i in range(pp)]
    in_specs += [page_spec((FOX_HEADS, page), p_i) for p_i in range(pp)]
    grid_spec = pltpu.PrefetchScalarGridSpec(
        num_scalar_prefetch=1,
        grid=(b, nsteps),
        in_specs=in_specs,
        out_specs=pl.BlockSpec((None, nrow, FOX_KV), lambda bi, j, pt: (bi, 0, 0)),
        scratch_shapes=[pltpu.VMEM((nrow, LANES), F32), pltpu.VMEM((nrow, LANES), F32),
                        pltpu.VMEM((nrow, FOX_KV), F32), pltpu.VMEM((FOX_HEADS, LANES), F32)],
    )
    return pl.pallas_call(
        functools.partial(_fox_decode_kernel, n_tok),
        grid_spec=grid_spec,
        out_shape=jax.ShapeDtypeStruct((b, nrow, FOX_KV), F32),
        compiler_params=_params("parallel", "arbitrary"),
        name="fox_decode",
    )(page_table, q_all, cq, k_new, v_new, kt_new, *([cache_k] * pp), *([cache_v] * pp), *([cache_lft] * pp))


def _lane_slab(pairs):
    out = jnp.zeros((1, LANES), F32)
    for off, vec in pairs:
        out = out.at[0, off:off + vec.shape[0]].set(vec.astype(F32))
    return out


def _expand_matrix(lane0, n_heads, width):
    rows = jnp.arange(LANES)[:, None]
    cols = jnp.arange(n_heads * width)[None, :]
    return jnp.where(rows == lane0 + cols // width, 1.0, 0.0).astype(BF16)


def _hybrid_weights(w_in):
    o = 0
    z = w_in[:, o:o + SSD_INNER]; o += SSD_INNER
    xbc = w_in[:, o:o + SSD_CONV_DIM]; o += SSD_CONV_DIM
    dt = w_in[:, o:o + SSD_HEADS]; o += SSD_HEADS
    qkv = w_in[:, o:o + GDN_CONV_DIM]; o += GDN_CONV_DIM
    a = w_in[:, o:o + GDN_HEADS]; o += GDN_HEADS
    bt = w_in[:, o:o + GDN_HEADS]; o += GDN_HEADS
    gate = w_in[:, o:o + GDN_VAL]
    main = jnp.concatenate([qkv, xbc[:, :SSD_INNER], z, gate, xbc[:, SSD_INNER:]], axis=1).astype(BF16)
    tail = jnp.concatenate([dt, a, bt, jnp.zeros((w_in.shape[0], LANES - SSD_HEADS - 2 * GDN_HEADS), w_in.dtype)],
                           axis=1).astype(BF16)
    return main, tail


_QKV_BLK = 0
_X_BLK = GDN_CONV_DIM // SSD_INNER
_Z_BLK = _X_BLK + 1
_GATE_BLK = _X_BLK + 2
_BC_BLK = (GDN_CONV_DIM + 3 * SSD_INNER) // SSD_BC


def _hybrid_layer(h_prompt, h_sample, state_ssd, state_ssd_conv, state_gdn, state_gdn_conv, p):
    b, l, d = h_prompt.shape
    db, t, _ = h_sample.shape
    w_main, w_tail = _hybrid_weights(p["w_in"])
    fm = w_main.shape[1]

    ssd_w = [p["ssd_conv_w"][:, :SSD_INNER], p["ssd_conv_w"][:, SSD_INNER:],
             p["ssd_conv_b"][None, :SSD_INNER], p["ssd_conv_b"][None, SSD_INNER:],
             _lane_slab([(DT_LANE, p["ssd_dt_bias"])]), _lane_slab([(DT_LANE, p["ssd_a_log"])]),
             jnp.repeat(p["ssd_d"], SSD_HEAD_DIM)[None, :], p["ssd_norm"][None, :],
             _expand_matrix(DT_LANE, SSD_HEADS, SSD_HEAD_DIM)]
    gdn_w = [p["gdn_conv_w"], _lane_slab([(GA_LANE, p["gdn_dt_bias"])]), _lane_slab([(GA_LANE, p["gdn_a_log"])]),
             p["gdn_norm"][None, :], _expand_matrix(GA_LANE, GDN_HEADS, GDN_DK),
             _expand_matrix(GB_LANE, GDN_HEADS, GDN_DK)]

    pm, pt = _rms_matmul(h_prompt.reshape(b * l, d), p["norm"], w_main, w_tail, 512)
    pm = pm.reshape(b, l, fm)
    pt = pt.reshape(b, l, LANES)
    y_p, ssd_state_p = _ssd_scan((pm, _X_BLK), (pm, _BC_BLK), (pm, _Z_BLK), (pt, 0), ssd_w, 0, l, None)
    o_p, gdn_state_p = _gdn_scan((pm, _QKV_BLK), (pm, _GATE_BLK), (pt, 0), gdn_w, 0, l, None)
    x0, bc0 = GDN_CONV_DIM, GDN_CONV_DIM + 3 * SSD_INNER
    ssd_conv_p = jnp.concatenate([pm[:, l - (CONV_K - 1):, x0:x0 + SSD_INNER], pm[:, l - (CONV_K - 1):, bc0:]],
                                 axis=-1)
    gdn_conv_p = pm[:, l - (CONV_K - 1):, :GDN_CONV_DIM]

    sm_, st_ = _rms_matmul(h_sample.reshape(db * t, d), p["norm"], w_main, w_tail, 512)
    sm_ = sm_.reshape(db, t, fm)
    st_ = st_.reshape(db, t, LANES)
    hist = jnp.zeros((db, CONV_K - 1, fm), F32)
    hist = hist.at[:, :, :GDN_CONV_DIM].set(state_gdn_conv)
    hist = hist.at[:, :, x0:x0 + SSD_INNER].set(state_ssd_conv[:, :, :SSD_INNER])
    hist = hist.at[:, :, bc0:].set(state_ssd_conv[:, :, SSD_INNER:])
    rows = SCAN_ROWS
    assert t + CONV_K - 1 <= CHUNK
    first = rows - t
    seq_m = jnp.concatenate([jnp.zeros((db, first - (CONV_K - 1), fm), F32), hist, sm_], axis=1)
    seq_t = jnp.concatenate([jnp.zeros((db, first, LANES), F32), st_], axis=1)
    y_s, ssd_state_s = _ssd_scan((seq_m, _X_BLK), (seq_m, _BC_BLK), (seq_m, _Z_BLK), (seq_t, 0), ssd_w,
                                 first, rows, state_ssd.reshape(db, SSD_INNER, SSD_STATE))
    o_s, gdn_state_s = _gdn_scan((seq_m, _QKV_BLK), (seq_m, _GATE_BLK), (seq_t, 0), gdn_w, first, rows, state_gdn)
    full = seq_m[:, rows - (CONV_K - 1):, :]
    ssd_conv_s = jnp.concatenate([full[:, :, x0:x0 + SSD_INNER], full[:, :, bc0:]], axis=-1)
    gdn_conv_s = full[:, :, :GDN_CONV_DIM]

    mix_p = [y_p.reshape(b * l, SSD_INNER), o_p.reshape(b * l, GDN_VAL)]
    mix_s = [y_s[:, first:].reshape(db * t, SSD_INNER), o_s[:, first:].reshape(db * t, GDN_VAL)]
    states = (ssd_state_p.reshape(b, SSD_HEADS, SSD_HEAD_DIM, SSD_STATE),
              ssd_state_s.reshape(db, SSD_HEADS, SSD_HEAD_DIM, SSD_STATE),
              ssd_conv_p, ssd_conv_s, gdn_state_p, gdn_state_s, gdn_conv_p, gdn_conv_s)
    return mix_p, mix_s, states


def _fox_layer(h_prompt, h_sample, cache_k, cache_v, cache_logf, page_table, p):
    b, l, d = h_prompt.shape
    db, t, _ = h_sample.shape
    w_in = p["w_in"]
    w_main = w_in[:, :FOX_Q + 2 * FOX_KV].astype(BF16)
    w_tail = jnp.concatenate([w_in[:, FOX_Q + 2 * FOX_KV:], jnp.zeros((d, LANES - FOX_HEADS), w_in.dtype)],
                             axis=1).astype(BF16)
    fm = w_main.shape[1]

    pm, pt = _rms_matmul(h_prompt.reshape(b * l, d), p["norm"], w_main, w_tail, 512)
    pm = pm.reshape(b, l, fm)
    qn, kn, kb, vb, lf, c, ct = _fox_prep(pm, pt.reshape(b, l, LANES), p["f_bias"], p["q_norm"], p["k_norm"],
                                          FOX_TILE, l)
    o_p = _fox_flash(qn, kb, vb, c, ct, FOX_TILE)
    k_p = kn.reshape(b, l, FOX_KV_HEADS, FOX_HEAD_DIM)
    v_p = pm[:, :, FOX_Q + FOX_KV:].reshape(b, l, FOX_KV_HEADS, FOX_HEAD_DIM)
    lf_p = lf[:, :, :FOX_HEADS]

    n = db * t
    sm_, st_ = _rms_matmul(h_sample.reshape(n, d), p["norm"], w_main, w_tail, 512)
    qn_s, kn_s, kb_s, vb_s, lf_s, c_s, _ = _fox_prep(sm_[None], st_[None], p["f_bias"], p["q_norm"], p["k_norm"],
                                                      n, t)
    k_s = kn_s.reshape(db, t, FOX_KV_HEADS, FOX_HEAD_DIM)
    v_s = sm_[:, FOX_Q + FOX_KV:].reshape(db, t, FOX_KV_HEADS, FOX_HEAD_DIM)
    lf_s = lf_s.reshape(db, t, LANES)[:, :, :FOX_HEADS]
    c_new = c_s.reshape(db, t, LANES)[:, :, :FOX_HEADS]
    nrow = t * FOX_HEADS
    page = cache_k.shape[1]
    group_of_head = jnp.arange(FOX_HEADS) // FOX_REP
    onehot = (group_of_head[:, None] == jnp.arange(FOX_KV_HEADS)[None, :]).astype(BF16)
    q_all = (qn_s.reshape(db, t, FOX_HEADS, 1, FOX_HEAD_DIM) * onehot[None, None, :, :, None]).reshape(
        db, nrow, FOX_KV)
    cq = jnp.broadcast_to(c_new.reshape(db, nrow, 1), (db, nrow, LANES))
    kt_new = jnp.zeros((db, t, FOX_HEADS, page), F32).at[:, :, :, :t].set(
        -jnp.broadcast_to(jnp.transpose(c_new, (0, 2, 1))[:, None], (db, t, FOX_HEADS, t))).reshape(db, nrow, page)
    k_pad = jnp.zeros((db, page, FOX_KV), BF16).at[:, :t].set(kb_s.reshape(db, t, FOX_KV))
    v_pad = jnp.zeros((db, page, FOX_KV), BF16).at[:, :t].set(vb_s.reshape(db, t, FOX_KV))
    o_all = _fox_decode(page_table, q_all, cq, k_pad, v_pad, kt_new,
                        cache_k.reshape(cache_k.shape[0], page, FOX_KV),
                        cache_v.reshape(cache_v.shape[0], page, FOX_KV),
                        jnp.transpose(cache_logf, (0, 2, 1)), t)
    o_sel = o_all.reshape(db, t, FOX_HEADS, FOX_KV_HEADS, FOX_HEAD_DIM)
    o_s = jnp.einsum("bthgd,hg->bthd", o_sel, onehot.astype(F32)).reshape(n, FOX_Q).astype(BF16)

    return [o_p.reshape(b * l, FOX_Q)], [o_s], (k_p, k_s, v_p, v_s, lf_p, lf_s)


def kernel(x_prompt, x_sample, state_ssd, state_ssd_conv, state_gdn, state_gdn_conv, cache_k, cache_v,
           cache_logf, page_table, meta_tokens, norm_mix, norm_mlp, w_in_hyb, ssd_conv_w, ssd_conv_b,
           ssd_dt_bias, ssd_a_log, ssd_d, ssd_norm, gdn_conv_w, gdn_dt_bias, gdn_a_log, gdn_norm, w_out_hyb,
           w_in_fox, fox_f_bias, fox_q_norm, fox_k_norm, w_out_fox, w_up, w_down):
    b, seq, d = x_prompt.shape
    db, t, _ = x_sample.shape
    l = N_META + seq
    hp = jnp.concatenate([jnp.broadcast_to(meta_tokens.astype(x_prompt.dtype)[None], (b, N_META, d)), x_prompt],
                         axis=1)
    hs = x_sample

    prm = dict(w_in=w_in_hyb[0], norm=norm_mix[0], ssd_conv_w=ssd_conv_w[0], ssd_conv_b=ssd_conv_b[0],
               ssd_dt_bias=ssd_dt_bias[0], ssd_a_log=ssd_a_log[0], ssd_d=ssd_d[0], ssd_norm=ssd_norm[0],
               gdn_conv_w=gdn_conv_w[0], gdn_dt_bias=gdn_dt_bias[0], gdn_a_log=gdn_a_log[0],
               gdn_norm=gdn_norm[0])
    mix_p, mix_s, states = _hybrid_layer(hp, hs, state_ssd[0], state_ssd_conv[0], state_gdn[0],
                                         state_gdn_conv[0], prm)
    w_out = w_out_hyb[0].astype(BF16)
    w_outs = [w_out[:SSD_INNER], w_out[SSD_INNER:]]
    wu, wd = w_up[0].astype(BF16), w_down[0].astype(BF16)
    hp = _mix_mlp(hp.reshape(b * l, d), mix_p, w_outs, norm_mlp[0], wu, wd).reshape(b, l, d)
    hs = _mix_mlp(hs.reshape(db * t, d), mix_s, w_outs, norm_mlp[0], wu, wd).reshape(db, t, d)

    prm = dict(w_in=w_in_fox[0], norm=norm_mix[1], f_bias=fox_f_bias[0], q_norm=fox_q_norm[0],
               k_norm=fox_k_norm[0])
    mix_p, mix_s, kv = _fox_layer(hp, hs, cache_k[0], cache_v[0], cache_logf[0], page_table, prm)
    w_outs = [w_out_fox[0].astype(BF16)]
    wu, wd = w_up[1].astype(BF16), w_down[1].astype(BF16)
    hp = _mix_mlp(hp.reshape(b * l, d), mix_p, w_outs, norm_mlp[1], wu, wd).reshape(b, l, d)
    hs = _mix_mlp(hs.reshape(db * t, d), mix_s, w_outs, norm_mlp[1], wu, wd).reshape(db, t, d)

    (ssd_p, ssd_s, sconv_p, sconv_s, gdn_p, gdn_s, gconv_p, gconv_s) = states
    k_p, k_s, v_p, v_s, lf_p, lf_s = kv
    return (hp[:, N_META:], hs, ssd_p[None], ssd_s[None], sconv_p[None], sconv_s[None], gdn_p[None],
            gdn_s[None], gconv_p[None], gconv_s[None], k_p[None], k_s[None], v_p[None], v_s[None],
            lf_p[None], lf_s[None])
```
